```python
import jax, jax.numpy as jnp
from jax import lax
import numpy as np

D_MODEL = 1024
BATCH = 1
SEQ = 16384
DEPTH = 2

D_CONF = 512
CONF_KERNEL = 31
D_SHORT = 512
SHORT_KERNEL = 3
D_FF_DENSE = 2816
N_EXPERTS = 8
TOP_K = 2
D_FF_EXPERT = 3584
RMS_EPS = 1e-6
LN_EPS = 1e-5
IN_COLS = 2 * D_CONF + 3 * D_SHORT + 2 * D_MODEL

kernel_name = "hybrid_conformer_shortconv_moe_trunk"


def rmsnorm(x, g):
    xf = x.astype(jnp.float32)
    y = xf * lax.rsqrt(jnp.mean(xf * xf, axis=-1, keepdims=True) + RMS_EPS)
    return (y * g.astype(jnp.float32)).astype(x.dtype)


def layernorm(x, g, b):
    xf = x.astype(jnp.float32)
    mu = jnp.mean(xf, axis=-1, keepdims=True)
    var = jnp.mean(jnp.square(xf - mu), axis=-1, keepdims=True)
    y = (xf - mu) * lax.rsqrt(var + LN_EPS)
    return (y * g.astype(jnp.float32) + b.astype(jnp.float32)).astype(x.dtype)


def causal_dwconv(x, w):
    k, c = w.shape
    return lax.conv_general_dilated(
        x, w[:, None, :].astype(x.dtype), window_strides=(1,), padding=[(k - 1, 0)],
        dimension_numbers=("NWC", "WIO", "NWC"), feature_group_count=c)


def token_mixer(h, w_in, conv_a, conv_a_bias, ln_a_g, ln_a_b, w_a, conv_b, w_b, w_o):
    z = h @ w_in
    splits = np.cumsum([D_CONF, D_CONF, D_SHORT, D_SHORT, D_SHORT, D_MODEL]).tolist()
    a_val, a_gate, s_b, s_c, s_x, g_a, g_b = jnp.split(z, splits, axis=-1)
    a = a_val * jax.nn.sigmoid(a_gate)
    a = causal_dwconv(a, conv_a) + conv_a_bias
    a = jax.nn.silu(layernorm(a, ln_a_g, ln_a_b))
    y_a = a @ w_a
    u = causal_dwconv(s_c * s_x, conv_b)
    y_b = (s_b * u) @ w_b
    merged = jax.nn.sigmoid(g_a) * y_a + jax.nn.sigmoid(g_b) * y_b
    return merged @ w_o


def swiglu(h, w1, w3, w2):
    return (jax.nn.silu(h @ w1) * (h @ w3)) @ w2


def moe_ffn(h, router, we1, we3, we2):
    logits = (h @ router).astype(jnp.float32)
    top_vals, top_idx = lax.top_k(logits, TOP_K)
    top_w = jax.nn.softmax(top_vals, axis=-1)
    combine = jnp.sum(jax.nn.one_hot(top_idx, N_EXPERTS, dtype=jnp.float32) * top_w[..., None], axis=-2)
    combine = combine.astype(h.dtype)
    out = jnp.zeros_like(h)
    for e in range(N_EXPERTS):
        out = out + combine[..., e:e + 1] * swiglu(h, we1[e], we3[e], we2[e])
    return out


def setup_inputs(seed: int = 0) -> dict:
    key = jax.random.key(seed)
    keys = iter(jax.random.split(key, 64))
    nrm = lambda shape, scale: jax.random.normal(next(keys), shape, jnp.float32) * scale
    gain = lambda n: 1.0 + nrm((n,), 0.02)
    inp = {"x": nrm((BATCH, SEQ, D_MODEL), 1.0)}
    for l in range(DEPTH):
        p = f"l{l}_"
        inp[p + "norm1"] = gain(D_MODEL)
        inp[p + "w_in"] = nrm((D_MODEL, IN_COLS), D_MODEL ** -0.5)
        inp[p + "conv_a"] = nrm((CONF_KERNEL, D_CONF), CONF_KERNEL ** -0.5)
        inp[p + "conv_a_bias"] = nrm((D_CONF,), 0.02)
        inp[p + "ln_a_g"] = gain(D_CONF)
        inp[p + "ln_a_b"] = nrm((D_CONF,), 0.02)
        inp[p + "w_a"] = nrm((D_CONF, D_MODEL), D_CONF ** -0.5)
        inp[p + "conv_b"] = nrm((SHORT_KERNEL, D_SHORT), SHORT_KERNEL ** -0.5)
        inp[p + "w_b"] = nrm((D_SHORT, D_MODEL), D_SHORT ** -0.5)
        inp[p + "w_o"] = nrm((D_MODEL, D_MODEL), D_MODEL ** -0.5)
        inp[p + "norm2"] = gain(D_MODEL)
        if l % 2 == 0:
            inp[p + "ffn_w1"] = nrm((D_MODEL, D_FF_DENSE), D_MODEL ** -0.5)
            inp[p + "ffn_w3"] = nrm((D_MODEL, D_FF_DENSE), D_MODEL ** -0.5)
            inp[p + "ffn_w2"] = nrm((D_FF_DENSE, D_MODEL), D_FF_DENSE ** -0.5)
        else:
            inp[p + "router"] = nrm((D_MODEL, N_EXPERTS), D_MODEL ** -0.5)
            inp[p + "moe_w1"] = nrm((N_EXPERTS, D_MODEL, D_FF_EXPERT), D_MODEL ** -0.5)
            inp[p + "moe_w3"] = nrm((N_EXPERTS, D_MODEL, D_FF_EXPERT), D_MODEL ** -0.5)
            inp[p + "moe_w2"] = nrm((N_EXPERTS, D_FF_EXPERT, D_MODEL), D_FF_EXPERT ** -0.5)
    inp["final_norm"] = gain(D_MODEL)
    return inp


def reference(x,
              l0_norm1, l0_w_in, l0_conv_a, l0_conv_a_bias, l0_ln_a_g, l0_ln_a_b, l0_w_a, l0_conv_b, l0_w_b, l0_w_o,
              l0_norm2, l0_ffn_w1, l0_ffn_w3, l0_ffn_w2,
              l1_norm1, l1_w_in, l1_conv_a, l1_conv_a_bias, l1_ln_a_g, l1_ln_a_b, l1_w_a, l1_conv_b, l1_w_b, l1_w_o,
              l1_norm2, l1_router, l1_moe_w1, l1_moe_w3, l1_moe_w2,
              final_norm):
    mixer_params = [
        (l0_w_in, l0_conv_a, l0_conv_a_bias, l0_ln_a_g, l0_ln_a_b, l0_w_a, l0_conv_b, l0_w_b, l0_w_o),
        (l1_w_in, l1_conv_a, l1_conv_a_bias, l1_ln_a_g, l1_ln_a_b, l1_w_a, l1_conv_b, l1_w_b, l1_w_o),
    ]
    norm1 = [l0_norm1, l1_norm1]
    norm2 = [l0_norm2, l1_norm2]
    dense_params = [(l0_ffn_w1, l0_ffn_w3, l0_ffn_w2)]
    moe_params = [(l1_router, l1_moe_w1, l1_moe_w3, l1_moe_w2)]
    for layer in range(DEPTH):
        x = x + token_mixer(rmsnorm(x, norm1[layer]), *mixer_params[layer])
        h = rmsnorm(x, norm2[layer])
        if layer % 2 == 0:
            x = x + swiglu(h, *dense_params[layer // 2])
        else:
            x = x + moe_ffn(h, *moe_params[layer // 2])
    return rmsnorm(x, final_norm)
```

```python
import functools

import jax
import jax.numpy as jnp
from jax import lax
from jax.experimental import pallas as pl
from jax.experimental.pallas import tpu as pltpu

D_MODEL = 1024
D_CONF = 512
CONF_KERNEL = 31
D_SHORT = 512
SHORT_KERNEL = 3
N_EXPERTS = 8
RMS_EPS = 1e-6
LN_EPS = 1e-5

_C_AV, _C_AG, _C_SB, _C_SC, _C_SX, _C_GA, _C_GB, _C_END = 0, 512, 1024, 1536, 2048, 2560, 3584, 4608

_VMEM_LIMIT = 56 * 1024 * 1024

_MIX_T = 512
_MIX_HALO = 32
_MIX_R = 64
_FFN_T = 512
_MOE_T = 1024
_MOE_F = 512

_BF = jnp.bfloat16
_F32 = jnp.float32


def _dot(a, b):
    return jnp.dot(a, b, preferred_element_type=_F32)


def _sigmoid(v):
    return 1.0 / (1.0 + jnp.exp(-v))


def _rms_scale(x, g):
    ms = jnp.mean(x * x, axis=-1, keepdims=True)
    return x * lax.rsqrt(ms + RMS_EPS) * g


def _const_spec(shape):
    return pl.BlockSpec(shape, lambda *_: (0,) * len(shape), pipeline_mode=pl.Buffered(1))


def _mixer_kernel(x_ref, n1_ref, win_ref, ca_ref, cab_ref, lng_ref, lnb_ref, wa_ref, cb_ref,
                  wb_ref, wo_ref, o_ref, abuf, vbuf, actbuf, ubuf):
    t = _MIX_T
    halo = _MIX_HALO

    @pl.when(pl.program_id(0) == 0)
    def _():
        abuf[0:halo, :] = jnp.zeros((halo, D_CONF), _F32)
        vbuf[0:8, :] = jnp.zeros((8, D_SHORT), _F32)

    x = x_ref[...]
    h = _rms_scale(x, n1_ref[...]).astype(_BF)

    za = _dot(h, win_ref[:, _C_AV:_C_SB])
    abuf[halo:halo + t, :] = za[:, :D_CONF] * _sigmoid(za[:, D_CONF:])
    zb = _dot(h, win_ref[:, _C_SB:_C_GA])
    vbuf[8:8 + t, :] = zb[:, D_SHORT:2 * D_SHORT] * zb[:, 2 * D_SHORT:]

    off_a = halo - (CONF_KERNEL - 1)
    off_b = 8 - (SHORT_KERNEL - 1)
    for c in range(t // _MIX_R):
        base = c * _MIX_R
        acc = jnp.broadcast_to(cab_ref[...], (_MIX_R, D_CONF))
        for k in range(CONF_KERNEL):
            s = base + off_a + k
            acc = acc + abuf[s:s + _MIX_R, :] * ca_ref[k:k + 1, :]
        mu = jnp.mean(acc, axis=-1, keepdims=True)
        cen = acc - mu
        var = jnp.mean(cen * cen, axis=-1, keepdims=True)
        y = cen * lax.rsqrt(var + LN_EPS) * lng_ref[...] + lnb_ref[...]
        actbuf[base:base + _MIX_R, :] = (y * _sigmoid(y)).astype(_BF)

        u = vbuf[base + off_b:base + off_b + _MIX_R, :] * cb_ref[0:1, :]
        for k in range(1, SHORT_KERNEL):
            s = base + off_b + k
            u = u + vbuf[s:s + _MIX_R, :] * cb_ref[k:k + 1, :]
        ubuf[base:base + _MIX_R, :] = (zb[base:base + _MIX_R, :D_SHORT] * u).astype(_BF)

    abuf[0:halo, :] = abuf[t:t + halo, :]
    vbuf[0:8, :] = vbuf[t:t + 8, :]

    y_a = _dot(actbuf[...], wa_ref[...])
    y_b = _dot(ubuf[...], wb_ref[...])
    zg = _dot(h, win_ref[:, _C_GA:_C_END])
    merged = _sigmoid(zg[:, :D_MODEL]) * y_a + _sigmoid(zg[:, D_MODEL:]) * y_b
    o_ref[...] = x + _dot(merged.astype(_BF), wo_ref[...])


def _mixer(x, norm1, w_in, conv_a, conv_a_bias, ln_g, ln_b, w_a, conv_b, w_b, w_o):
    s = x.shape[0]
    t = _MIX_T
    row = lambda v: v.reshape(1, -1).astype(_F32)
    return pl.pallas_call(
        _mixer_kernel,
        grid=(s // t,),
        in_specs=[
            pl.BlockSpec((t, D_MODEL), lambda i: (i, 0)),
            _const_spec((1, D_MODEL)),
            _const_spec((D_MODEL, _C_END)),
            _const_spec((CONF_KERNEL, D_CONF)),
            _const_spec((1, D_CONF)),
            _const_spec((1, D_CONF)),
            _const_spec((1, D_CONF)),
            _const_spec((D_CONF, D_MODEL)),
            _const_spec((SHORT_KERNEL, D_SHORT)),
            _const_spec((D_SHORT, D_MODEL)),
            _const_spec((D_MODEL, D_MODEL)),
        ],
        out_specs=pl.BlockSpec((t, D_MODEL), lambda i: (i, 0)),
        out_shape=jax.ShapeDtypeStruct((s, D_MODEL), _F32),
        scratch_shapes=[
            pltpu.VMEM((_MIX_HALO + t, D_CONF), _F32),
            pltpu.VMEM((8 + t, D_SHORT), _F32),
            pltpu.VMEM((t, D_CONF), _BF),
            pltpu.VMEM((t, D_SHORT), _BF),
        ],
        compiler_params=pltpu.CompilerParams(
            dimension_semantics=("arbitrary",), vmem_limit_bytes=_VMEM_LIMIT),
        name="mixer",
    )(x, row(norm1), w_in.astype(_BF), conv_a, row(conv_a_bias), row(ln_g), row(ln_b),
      w_a.astype(_BF), conv_b, w_b.astype(_BF), w_o.astype(_BF))


def _ffn_kernel(x_ref, n2_ref, w1_ref, w3_ref, w2_ref, o_ref):
    x = x_ref[...]
    h = _rms_scale(x, n2_ref[...]).astype(_BF)
    g = _dot(h, w1_ref[...])
    u = _dot(h, w3_ref[...])
    act = (g * _sigmoid(g) * u).astype(_BF)
    o_ref[...] = x + _dot(act, w2_ref[...])


def _ffn_dense(x, norm2, w1, w3, w2):
    s = x.shape[0]
    t = _FFN_T
    f = w1.shape[1]
    return pl.pallas_call(
        _ffn_kernel,
        grid=(s // t,),
        in_specs=[
            pl.BlockSpec((t, D_MODEL), lambda i: (i, 0)),
            _const_spec((1, D_MODEL)),
            _const_spec((D_MODEL, f)),
            _const_spec((D_MODEL, f)),
            _const_spec((f, D_MODEL)),
        ],
        out_specs=pl.BlockSpec((t, D_MODEL), lambda i: (i, 0)),
        out_shape=jax.ShapeDtypeStruct((s, D_MODEL), _F32),
        compiler_params=pltpu.CompilerParams(
            dimension_semantics=("arbitrary",), vmem_limit_bytes=_VMEM_LIMIT),
        name="ffn_dense",
    )(x, norm2.reshape(1, -1), w1.astype(_BF), w3.astype(_BF), w2.astype(_BF))


def _moe_kernel(x_ref, n2_ref, r_ref, w1_ref, w3_ref, w2_ref, fn_ref, o_ref, hbuf, comb, acc):
    e = pl.program_id(1)
    f = pl.program_id(2)

    @pl.when((e == 0) & (f == 0))
    def _():
        x = x_ref[...]
        h = _rms_scale(x, n2_ref[...])
        hbuf[...] = h.astype(_BF)
        logits = jnp.dot(h, r_ref[...], preferred_element_type=_F32,
                         precision=lax.Precision.HIGHEST)
        lane = lax.broadcasted_iota(jnp.int32, logits.shape, 1)
        m1 = jnp.max(logits, axis=-1, keepdims=True)
        i1 = jnp.min(jnp.where(logits == m1, lane, N_EXPERTS), axis=-1, keepdims=True)
        rest = jnp.where(lane == i1, -jnp.inf, logits)
        m2 = jnp.max(rest, axis=-1, keepdims=True)
        i2 = jnp.min(jnp.where(rest == m2, lane, N_EXPERTS), axis=-1, keepdims=True)
        tt = jnp.exp(m2 - m1)
        p1 = 1.0 / (1.0 + tt)
        p2 = tt / (1.0 + tt)
        comb[...] = jnp.where(lane == i1, p1, 0.0) + jnp.where(lane == i2, p2, 0.0)
        acc[...] = x

    h = hbuf[...]
    g = _dot(h, w1_ref[...])
    u = _dot(h, w3_ref[...])
    lane = lax.broadcasted_iota(jnp.int32, comb.shape, 1)
    c_e = jnp.sum(jnp.where(lane == e, comb[...], 0.0), axis=-1, keepdims=True)
    act = (g * _sigmoid(g) * u * c_e).astype(_BF)
    acc[...] += _dot(act, w2_ref[...])

    @pl.when((e == pl.num_programs(1) - 1) & (f == pl.num_programs(2) - 1))
    def _():
        o_ref[...] = _rms_scale(acc[...], fn_ref[...])


def _moe_dense(x, norm2, router, w1, w3, w2, final_norm):
    s = x.shape[0]
    t = _MOE_T
    fc = _MOE_F
    ne, _, ff = w1.shape
    return pl.pallas_call(
        _moe_kernel,
        grid=(s // t, ne, ff // fc),
        in_specs=[
            pl.BlockSpec((t, D_MODEL), lambda i, e, f: (i, 0)),
            pl.BlockSpec((1, D_MODEL), lambda i, e, f: (0, 0)),
            pl.BlockSpec((D_MODEL, ne), lambda i, e, f: (0, 0)),
            pl.BlockSpec((None, D_MODEL, fc), lambda i, e, f: (e, 0, f)),
            pl.BlockSpec((None, D_MODEL, fc), lambda i, e, f: (e, 0, f)),
            pl.BlockSpec((None, fc, D_MODEL), lambda i, e, f: (e, f, 0)),
            pl.BlockSpec((1, D_MODEL), lambda i, e, f: (0, 0)),
        ],
        out_specs=pl.BlockSpec((t, D_MODEL), lambda i, e, f: (i, 0)),
        out_shape=jax.ShapeDtypeStruct((s, D_MODEL), _F32),
        scratch_shapes=[
            pltpu.VMEM((t, D_MODEL), _BF),
            pltpu.VMEM((t, ne), _F32),
            pltpu.VMEM((t, D_MODEL), _F32),
        ],
        compiler_params=pltpu.CompilerParams(
            dimension_semantics=("arbitrary", "arbitrary", "arbitrary"),
            vmem_limit_bytes=_VMEM_LIMIT),
        name="moe_dense",
    )(x, norm2.reshape(1, -1), router, w1.astype(_BF), w3.astype(_BF), w2.astype(_BF),
      final_norm.reshape(1, -1))


def kernel(x, l0_norm1, l0_w_in, l0_conv_a, l0_conv_a_bias, l0_ln_a_g, l0_ln_a_b, l0_w_a, l0_conv_b, l0_w_b, l0_w_o, l0_norm2, l0_ffn_w1, l0_ffn_w3, l0_ffn_w2, l1_norm1, l1_w_in, l1_conv_a, l1_conv_a_bias, l1_ln_a_g, l1_ln_a_b, l1_w_a, l1_conv_b, l1_w_b, l1_w_o, l1_norm2, l1_router, l1_moe_w1, l1_moe_w3, l1_moe_w2, final_norm):
    b, s, d = x.shape
    xs = x.reshape(b * s, d)
    xs = _mixer(xs, l0_norm1, l0_w_in, l0_conv_a, l0_conv_a_bias, l0_ln_a_g, l0_ln_a_b, l0_w_a,
                l0_conv_b, l0_w_b, l0_w_o)
    xs = _ffn_dense(xs, l0_norm2, l0_ffn_w1, l0_ffn_w3, l0_ffn_w2)
    xs = _mixer(xs, l1_norm1, l1_w_in, l1_conv_a, l1_conv_a_bias, l1_ln_a_g, l1_ln_a_b, l1_w_a,
                l1_conv_b, l1_w_b, l1_w_o)
    out = _moe_dense(xs, l1_norm2, l1_router, l1_moe_w1, l1_moe_w3, l1_moe_w2, final_norm)
    return out.reshape(b, s, d)
```

```python
import functools

import jax
import jax.numpy as jnp
from jax import lax
from jax.experimental import pallas as pl
from jax.experimental.pallas import tpu as pltpu

D_MODEL = 1024
D_CONF = 512
CONF_KERNEL = 31
D_SHORT = 512
SHORT_KERNEL = 3
N_EXPERTS = 8
TOP_K = 2
RMS_EPS = 1e-6
LN_EPS = 1e-5

_C_AV, _C_AG, _C_SB, _C_SC, _C_SX, _C_GA, _C_GB, _C_END = 0, 512, 1024, 1536, 2048, 2560, 3584, 4608

_VMEM_LIMIT = 56 * 1024 * 1024

_MIX_T = 512
_MIX_HALO = 32
_MIX_R = 64
_FFN_T = 512
_RT_T = 512
_DSP_T = 512
_EXP_TM = 512
_EXP_F = 512
_CMB_T = 512

_BF = jnp.bfloat16
_F32 = jnp.float32


def _dot(a, b):
    return jnp.dot(a, b, preferred_element_type=_F32)


def _sigmoid(v):
    return 1.0 / (1.0 + jnp.exp(-v))


def _rms_scale(x, g):
    ms = jnp.mean(x * x, axis=-1, keepdims=True)
    return x * lax.rsqrt(ms + RMS_EPS) * g


def _const_spec(shape):
    return pl.BlockSpec(shape, lambda *_: (0,) * len(shape), pipeline_mode=pl.Buffered(1))


def _mixer_kernel(x_ref, n1_ref, win_ref, ca_ref, cab_ref, lng_ref, lnb_ref, wa_ref, cb_ref,
                  wb_ref, wo_ref, o_ref, abuf, vbuf, actbuf, ubuf):
    t = _MIX_T
    halo = _MIX_HALO

    @pl.when(pl.program_id(0) == 0)
    def _():
        abuf[0:halo, :] = jnp.zeros((halo, D_CONF), _F32)
        vbuf[0:8, :] = jnp.zeros((8, D_SHORT), _F32)

    x = x_ref[...]
    h = _rms_scale(x, n1_ref[...]).astype(_BF)

    za = _dot(h, win_ref[:, _C_AV:_C_SB])
    abuf[halo:halo + t, :] = za[:, :D_CONF] * _sigmoid(za[:, D_CONF:])
    zb = _dot(h, win_ref[:, _C_SB:_C_GA])
    vbuf[8:8 + t, :] = zb[:, D_SHORT:2 * D_SHORT] * zb[:, 2 * D_SHORT:]

    off_a = halo - (CONF_KERNEL - 1)
    off_b = 8 - (SHORT_KERNEL - 1)
    for c in range(t // _MIX_R):
        base = c * _MIX_R
        acc = jnp.broadcast_to(cab_ref[...], (_MIX_R, D_CONF))
        for k in range(CONF_KERNEL):
            s = base + off_a + k
            acc = acc + abuf[s:s + _MIX_R, :] * ca_ref[k:k + 1, :]
        mu = jnp.mean(acc, axis=-1, keepdims=True)
        cen = acc - mu
        var = jnp.mean(cen * cen, axis=-1, keepdims=True)
        y = cen * lax.rsqrt(var + LN_EPS) * lng_ref[...] + lnb_ref[...]
        actbuf[base:base + _MIX_R, :] = (y * _sigmoid(y)).astype(_BF)

        u = vbuf[base + off_b:base + off_b + _MIX_R, :] * cb_ref[0:1, :]
        for k in range(1, SHORT_KERNEL):
            s = base + off_b + k
            u = u + vbuf[s:s + _MIX_R, :] * cb_ref[k:k + 1, :]
        ubuf[base:base + _MIX_R, :] = (zb[base:base + _MIX_R, :D_SHORT] * u).astype(_BF)

    abuf[0:halo, :] = abuf[t:t + halo, :]
    vbuf[0:8, :] = vbuf[t:t + 8, :]

    y_a = _dot(actbuf[...], wa_ref[...])
    y_b = _dot(ubuf[...], wb_ref[...])
    zg = _dot(h, win_ref[:, _C_GA:_C_END])
    merged = _sigmoid(zg[:, :D_MODEL]) * y_a + _sigmoid(zg[:, D_MODEL:]) * y_b
    o_ref[...] = x + _dot(merged.astype(_BF), wo_ref[...])


def _mixer(x, norm1, w_in, conv_a, conv_a_bias, ln_g, ln_b, w_a, conv_b, w_b, w_o):
    s = x.shape[0]
    t = _MIX_T
    row = lambda v: v.reshape(1, -1).astype(_F32)
    return pl.pallas_call(
        _mixer_kernel,
        grid=(s // t,),
        in_specs=[
            pl.BlockSpec((t, D_MODEL), lambda i: (i, 0)),
            _const_spec((1, D_MODEL)),
            _const_spec((D_MODEL, _C_END)),
            _const_spec((CONF_KERNEL, D_CONF)),
            _const_spec((1, D_CONF)),
            _const_spec((1, D_CONF)),
            _const_spec((1, D_CONF)),
            _const_spec((D_CONF, D_MODEL)),
            _const_spec((SHORT_KERNEL, D_SHORT)),
            _const_spec((D_SHORT, D_MODEL)),
            _const_spec((D_MODEL, D_MODEL)),
        ],
        out_specs=pl.BlockSpec((t, D_MODEL), lambda i: (i, 0)),
        out_shape=jax.ShapeDtypeStruct((s, D_MODEL), _F32),
        scratch_shapes=[
            pltpu.VMEM((_MIX_HALO + t, D_CONF), _F32),
            pltpu.VMEM((8 + t, D_SHORT), _F32),
            pltpu.VMEM((t, D_CONF), _BF),
            pltpu.VMEM((t, D_SHORT), _BF),
        ],
        compiler_params=pltpu.CompilerParams(
            dimension_semantics=("arbitrary",), vmem_limit_bytes=_VMEM_LIMIT),
        name="mixer",
    )(x, row(norm1), w_in.astype(_BF), conv_a, row(conv_a_bias), row(ln_g), row(ln_b),
      w_a.astype(_BF), conv_b, w_b.astype(_BF), w_o.astype(_BF))


def _ffn_kernel(x_ref, n2_ref, w1_ref, w3_ref, w2_ref, o_ref):
    x = x_ref[...]
    h = _rms_scale(x, n2_ref[...]).astype(_BF)
    g = _dot(h, w1_ref[...])
    u = _dot(h, w3_ref[...])
    act = (g * _sigmoid(g) * u).astype(_BF)
    o_ref[...] = x + _dot(act, w2_ref[...])


def _ffn_dense(x, norm2, w1, w3, w2):
    s = x.shape[0]
    t = _FFN_T
    f = w1.shape[1]
    return pl.pallas_call(
        _ffn_kernel,
        grid=(s // t,),
        in_specs=[
            pl.BlockSpec((t, D_MODEL), lambda i: (i, 0)),
            _const_spec((1, D_MODEL)),
            _const_spec((D_MODEL, f)),
            _const_spec((D_MODEL, f)),
            _const_spec((f, D_MODEL)),
        ],
        out_specs=pl.BlockSpec((t, D_MODEL), lambda i: (i, 0)),
        out_shape=jax.ShapeDtypeStruct((s, D_MODEL), _F32),
        compiler_params=pltpu.CompilerParams(
            dimension_semantics=("arbitrary",), vmem_limit_bytes=_VMEM_LIMIT),
        name="ffn_dense",
    )(x, norm2.reshape(1, -1), w1.astype(_BF), w3.astype(_BF), w2.astype(_BF))


def _route_kernel(x_ref, n2_ref, r_ref, h_ref, meta_ref, cnt_ref, lmat, carry):
    t = _RT_T

    @pl.when(pl.program_id(0) == 0)
    def _():
        row = lax.broadcasted_iota(jnp.int32, (t, t), 0)
        col = lax.broadcasted_iota(jnp.int32, (t, t), 1)
        lmat[...] = jnp.where(col < row, 1.0, 0.0).astype(_BF)
        carry[...] = jnp.zeros(carry.shape, _F32)

    h = _rms_scale(x_ref[...], n2_ref[...])
    h_ref[...] = h
    logits = jnp.dot(h, r_ref[...], preferred_element_type=_F32, precision=lax.Precision.HIGHEST)
    lane = lax.broadcasted_iota(jnp.int32, logits.shape, 1)
    m1 = jnp.max(logits, axis=-1, keepdims=True)
    i1 = jnp.min(jnp.where(logits == m1, lane, N_EXPERTS), axis=-1, keepdims=True)
    rest = jnp.where(lane == i1, -jnp.inf, logits)
    m2 = jnp.max(rest, axis=-1, keepdims=True)
    i2 = jnp.min(jnp.where(rest == m2, lane, N_EXPERTS), axis=-1, keepdims=True)
    tt = jnp.exp(m2 - m1)
    p1 = 1.0 / (1.0 + tt)
    p2 = tt / (1.0 + tt)

    sel1 = lane == i1
    sel2 = lane == i2
    onehot = jnp.where(sel1, 1.0, 0.0) + jnp.where(sel2, 1.0, 0.0)
    before = _dot(lmat[...], onehot.astype(_BF)) + carry[...]
    rank1 = jnp.sum(jnp.where(sel1, before, 0.0), axis=-1, keepdims=True)
    rank2 = jnp.sum(jnp.where(sel2, before, 0.0), axis=-1, keepdims=True)
    cols = (rank1, rank2, p1, p2, i1.astype(_F32), i2.astype(_F32))
    meta = jnp.zeros(logits.shape, _F32)
    for c, v in enumerate(cols):
        meta = jnp.where(lane == c, v, meta)
    meta_ref[...] = meta
    carry[...] += jnp.sum(onehot, axis=0, keepdims=True)
    cnt_ref[...] = carry[...]


def _route(x, norm2, router):
    s = x.shape[0]
    t = _RT_T
    return pl.pallas_call(
        _route_kernel,
        grid=(s // t,),
        in_specs=[
            pl.BlockSpec((t, D_MODEL), lambda i: (i, 0)),
            _const_spec((1, D_MODEL)),
            _const_spec((D_MODEL, N_EXPERTS)),
        ],
        out_specs=[
            pl.BlockSpec((t, D_MODEL), lambda i: (i, 0)),
            pl.BlockSpec((t, N_EXPERTS), lambda i: (i, 0)),
            pl.BlockSpec((1, N_EXPERTS), lambda i: (0, 0)),
        ],
        out_shape=[
            jax.ShapeDtypeStruct((s, D_MODEL), _F32),
            jax.ShapeDtypeStruct((s, N_EXPERTS), _F32),
            jax.ShapeDtypeStruct((1, N_EXPERTS), _F32),
        ],
        scratch_shapes=[pltpu.VMEM((t, t), _BF), pltpu.VMEM((1, N_EXPERTS), _F32)],
        compiler_params=pltpu.CompilerParams(
            dimension_semantics=("arbitrary",), vmem_limit_bytes=_VMEM_LIMIT),
        name="route",
    )(x, norm2.reshape(1, -1), router)


def _dispatch_kernel(plan_ref, slot_hbm, h_ref, xs_hbm, slot_smem, zbuf, sem_idx, sem_row, sem_z,
                     *, max_tiles):
    t = _DSP_T
    tm = _EXP_TM
    i = pl.program_id(0)
    idx_copy = pltpu.make_async_copy(
        slot_hbm.at[pl.ds(i * TOP_K * t, TOP_K * t)], slot_smem, sem_idx)
    idx_copy.start()

    @pl.when(i == 0)
    def _():
        zbuf[...] = jnp.zeros(zbuf.shape, _F32)

        def zero_row(r, carry):
            pltpu.make_async_copy(zbuf.at[pl.ds(0, 1), :], xs_hbm.at[pl.ds(r, 1), :], sem_z).start()
            return carry

        def wait_row(r, carry):
            pltpu.make_async_copy(zbuf.at[pl.ds(0, 1), :], xs_hbm.at[pl.ds(0, 1), :], sem_z).wait()
            return carry

        def zero_tile(j, carry):
            pltpu.make_async_copy(
                zbuf, xs_hbm.at[pl.ds(pl.multiple_of(j * tm, tm), tm), :], sem_z).start()
            return carry

        def wait_tile(j, carry):
            pltpu.make_async_copy(zbuf, xs_hbm.at[pl.ds(0, tm), :], sem_z).wait()
            return carry

        for e in range(N_EXPERTS):
            lo = plan_ref[N_EXPERTS + e] + plan_ref[e]
            hi = plan_ref[N_EXPERTS + e] + ((plan_ref[e] + tm - 1) // tm) * tm
            lax.fori_loop(lo, hi, zero_row, 0)
            lax.fori_loop(lo, hi, wait_row, 0)
        n_tiles = plan_ref[2 * N_EXPERTS]
        lax.fori_loop(n_tiles, max_tiles, zero_tile, 0)
        lax.fori_loop(n_tiles, max_tiles, wait_tile, 0)

    idx_copy.wait()

    def issue(r, carry):
        for k in range(TOP_K):
            pltpu.make_async_copy(
                h_ref.at[pl.ds(r, 1), :], xs_hbm.at[pl.ds(slot_smem[TOP_K * r + k], 1), :],
                sem_row).start()
        return carry

    lax.fori_loop(0, t, issue, 0, unroll=8)
    for k in range(TOP_K):
        pltpu.make_async_copy(h_ref, xs_hbm.at[pl.ds(0, t), :], sem_row).wait()


def _dispatch(h, slots, plan, max_tiles):
    s = h.shape[0]
    t = _DSP_T
    grid_spec = pltpu.PrefetchScalarGridSpec(
        num_scalar_prefetch=1,
        grid=(s // t,),
        in_specs=[
            pl.BlockSpec(memory_space=pl.ANY),
            pl.BlockSpec((t, D_MODEL), lambda i, plan: (i, 0)),
        ],
        out_specs=pl.BlockSpec(memory_space=pl.ANY),
        scratch_shapes=[
            pltpu.SMEM((TOP_K * t,), jnp.int32),
            pltpu.VMEM((_EXP_TM, D_MODEL), _F32),
            pltpu.SemaphoreType.DMA,
            pltpu.SemaphoreType.DMA,
            pltpu.SemaphoreType.DMA,
        ],
    )
    return pl.pallas_call(
        functools.partial(_dispatch_kernel, max_tiles=max_tiles),
        grid_spec=grid_spec,
        out_shape=jax.ShapeDtypeStruct((max_tiles * _EXP_TM, D_MODEL), _F32),
        compiler_params=pltpu.CompilerParams(
            dimension_semantics=("arbitrary",), vmem_limit_bytes=_VMEM_LIMIT),
        name="dispatch",
    )(plan, slots, h)


def _experts_kernel(exp_ref, nt_ref, xs_ref, w1_ref, w3_ref, w2_ref, ys_ref, hbuf, acc):
    i = pl.program_id(0)
    f = pl.program_id(1)
    nf = pl.num_programs(1)

    @pl.when(i < nt_ref[0])
    def _():
        @pl.when(f == 0)
        def _():
            hbuf[...] = xs_ref[...].astype(_BF)

        h = hbuf[...]
        g = _dot(h, w1_ref[...])
        u = _dot(h, w3_ref[...])
        part = _dot((g * _sigmoid(g) * u).astype(_BF), w2_ref[...])

        @pl.when(f == 0)
        def _():
            acc[...] = part

        @pl.when(f > 0)
        def _():
            acc[...] += part

        @pl.when(f == nf - 1)
        def _():
            ys_ref[...] = acc[...]

    @pl.when((i >= nt_ref[0]) & (f == 0))
    def _():
        ys_ref[...] = jnp.zeros(ys_ref.shape, _F32)


def _experts(xs, w1, w3, w2, tile_exp, n_tiles, max_tiles):
    tm = _EXP_TM
    fc = _EXP_F
    ne, _, ff = w1.shape
    nf = ff // fc

    def xs_map(i, f, exp, nt):
        return (jnp.minimum(i, nt[0] - 1), 0)

    def w13_map(i, f, exp, nt):
        return (exp[i], 0, jnp.where(i < nt[0], f, nf - 1))

    def w2_map(i, f, exp, nt):
        return (exp[i], jnp.where(i < nt[0], f, nf - 1), 0)

    grid_spec = pltpu.PrefetchScalarGridSpec(
        num_scalar_prefetch=2,
        grid=(max_tiles, nf),
        in_specs=[
            pl.BlockSpec((tm, D_MODEL), xs_map),
            pl.BlockSpec((None, D_MODEL, fc), w13_map),
            pl.BlockSpec((None, D_MODEL, fc), w13_map),
            pl.BlockSpec((None, fc, D_MODEL), w2_map),
        ],
        out_specs=pl.BlockSpec((tm, D_MODEL), lambda i, f, exp, nt: (i, 0)),
        scratch_shapes=[pltpu.VMEM((tm, D_MODEL), _BF), pltpu.VMEM((tm, D_MODEL), _F32)],
    )
    return pl.pallas_call(
        _experts_kernel,
        grid_spec=grid_spec,
        out_shape=jax.ShapeDtypeStruct(xs.shape, _F32),
        compiler_params=pltpu.CompilerParams(
            dimension_semantics=("arbitrary", "arbitrary"), vmem_limit_bytes=_VMEM_LIMIT),
        name="experts",
    )(tile_exp, n_tiles, xs, w1.astype(_BF), w3.astype(_BF), w2.astype(_BF))


def _combine_kernel(slot_hbm, x_ref, meta_ref, fn_ref, ys_hbm, o_ref, slot_smem, gbuf, sem_idx,
                    sem_row):
    t = _CMB_T
    i = pl.program_id(0)
    idx_copy = pltpu.make_async_copy(
        slot_hbm.at[pl.ds(i * TOP_K * t, TOP_K * t)], slot_smem, sem_idx)
    idx_copy.start()
    idx_copy.wait()

    def issue(r, carry):
        for k in range(TOP_K):
            pltpu.make_async_copy(
                ys_hbm.at[pl.ds(slot_smem[TOP_K * r + k], 1), :], gbuf.at[k, pl.ds(r, 1), :],
                sem_row).start()
        return carry

    lax.fori_loop(0, t, issue, 0, unroll=8)
    for k in range(TOP_K):
        pltpu.make_async_copy(ys_hbm.at[pl.ds(0, t), :], gbuf.at[k], sem_row).wait()

    meta = meta_ref[...]
    y = x_ref[...] + meta[:, 2:3] * gbuf[0] + meta[:, 3:4] * gbuf[1]
    o_ref[...] = _rms_scale(y, fn_ref[...])


def _combine(x, meta, slots, ys, final_norm):
    s = x.shape[0]
    t = _CMB_T
    return pl.pallas_call(
        _combine_kernel,
        grid=(s // t,),
        in_specs=[
            pl.BlockSpec(memory_space=pl.ANY),
            pl.BlockSpec((t, D_MODEL), lambda i: (i, 0)),
            pl.BlockSpec((t, N_EXPERTS), lambda i: (i, 0)),
            _const_spec((1, D_MODEL)),
            pl.BlockSpec(memory_space=pl.ANY),
        ],
        out_specs=pl.BlockSpec((t, D_MODEL), lambda i: (i, 0)),
        out_shape=jax.ShapeDtypeStruct((s, D_MODEL), _F32),
        scratch_shapes=[
            pltpu.SMEM((TOP_K * t,), jnp.int32),
            pltpu.VMEM((TOP_K, t, D_MODEL), _F32),
            pltpu.SemaphoreType.DMA,
            pltpu.SemaphoreType.DMA,
        ],
        compiler_params=pltpu.CompilerParams(
            dimension_semantics=("arbitrary",), vmem_limit_bytes=_VMEM_LIMIT),
        name="combine",
    )(slots, x, meta, final_norm.reshape(1, -1), ys)


def _moe_routed(x, norm2, router, w1, w3, w2, final_norm):
    s = x.shape[0]
    tm = _EXP_TM
    max_tiles = (TOP_K * s) // tm + N_EXPERTS

    h, meta, counts = _route(x, norm2, router)

    counts = counts.reshape(N_EXPERTS).astype(jnp.int32)
    tiles_per_expert = (counts + tm - 1) // tm
    tile_end = jnp.cumsum(tiles_per_expert)
    row_start = (tile_end - tiles_per_expert) * tm
    n_tiles = tile_end[-1:]
    tile_id = jnp.minimum(jnp.arange(max_tiles, dtype=jnp.int32), n_tiles[0] - 1)
    tile_exp = jnp.sum(tile_id[:, None] >= tile_end[None, :], axis=1).astype(jnp.int32)
    slots = meta[:, :TOP_K].astype(jnp.int32) + row_start[meta[:, 4:4 + TOP_K].astype(jnp.int32)]
    slots = slots.reshape(TOP_K * s)
    plan = jnp.concatenate([counts, row_start, n_tiles]).astype(jnp.int32)

    xs = _dispatch(h, slots, plan, max_tiles)
    ys = _experts(xs, w1, w3, w2, tile_exp, n_tiles.astype(jnp.int32), max_tiles)
    return _combine(x, meta, slots, ys, final_norm)


def kernel(x, l0_norm1, l0_w_in, l0_conv_a, l0_conv_a_bias, l0_ln_a_g, l0_ln_a_b, l0_w_a, l0_conv_b, l0_w_b, l0_w_o, l0_norm2, l0_ffn_w1, l0_ffn_w3, l0_ffn_w2, l1_norm1, l1_w_in, l1_conv_a, l1_conv_a_bias, l1_ln_a_g, l1_ln_a_b, l1_w_a, l1_conv_b, l1_w_b, l1_w_o, l1_norm2, l1_router, l1_moe_w1, l1_moe_w3, l1_moe_w2, final_norm):
    b, s, d = x.shape
    assert b == 1 and d == D_MODEL, "the causal conv history is carried across token tiles of one sequence"
    xs = x.reshape(s, d)
    xs = _mixer(xs, l0_norm1, l0_w_in, l0_conv_a, l0_conv_a_bias, l0_ln_a_g, l0_ln_a_b, l0_w_a,
                l0_conv_b, l0_w_b, l0_w_o)
    xs = _ffn_dense(xs, l0_norm2, l0_ffn_w1, l0_ffn_w3, l0_ffn_w2)
    xs = _mixer(xs, l1_norm1, l1_w_in, l1_conv_a, l1_conv_a_bias, l1_ln_a_g, l1_ln_a_b, l1_w_a,
                l1_conv_b, l1_w_b, l1_w_o)
    out = _moe_routed(xs, l1_norm2, l1_router, l1_moe_w1, l1_moe_w3, l1_moe_w2, final_norm)
    return out.reshape(b, s, d)
```

```python
import functools

import jax
import jax.numpy as jnp
from jax import lax
from jax.experimental import pallas as pl
from jax.experimental.pallas import tpu as pltpu

D_MODEL = 1024
D_CONF = 512
CONF_KERNEL = 31
D_SHORT = 512
SHORT_KERNEL = 3
N_EXPERTS = 8
TOP_K = 2
RMS_EPS = 1e-6
LN_EPS = 1e-5

_C_AV, _C_AG, _C_SB, _C_SC, _C_SX, _C_GA, _C_GB, _C_END = 0, 512, 1024, 1536, 2048, 2560, 3584, 4608

_VMEM_LIMIT = 56 * 1024 * 1024

_MIX_T = 512
_MIX_HALO = 32
_MIX_R = 64
_FFN_T = 512
_RT_T = 512
_DSP_T = 512
_EXP_TM = 512
_EXP_F = 512
_CMB_T = 512

_BF = jnp.bfloat16
_F32 = jnp.float32


def _dot(a, b):
    return jnp.dot(a, b, preferred_element_type=_F32)


def _sigmoid(v):
    return 0.5 * jnp.tanh(0.5 * v) + 0.5


def _rms_scale(x, g):
    ms = jnp.mean(x * x, axis=-1, keepdims=True)
    return x * lax.rsqrt(ms + RMS_EPS) * g


def _const_spec(shape):
    return pl.BlockSpec(shape, lambda *_: (0,) * len(shape), pipeline_mode=pl.Buffered(1))


def _mixer_kernel(x_ref, n1_ref, win_ref, ca_ref, cab_ref, lng_ref, lnb_ref, wa_ref, cb_ref,
                  wb_ref, wo_ref, o_ref, abuf, vbuf, actbuf, ubuf):
    t = _MIX_T
    halo = _MIX_HALO

    @pl.when(pl.program_id(0) == 0)
    def _():
        abuf[0:halo, :] = jnp.zeros((halo, D_CONF), _F32)
        vbuf[0:8, :] = jnp.zeros((8, D_SHORT), _F32)

    x = x_ref[...]
    h = _rms_scale(x, n1_ref[...]).astype(_BF)

    za = _dot(h, win_ref[:, _C_AV:_C_SB])
    abuf[halo:halo + t, :] = za[:, :D_CONF] * _sigmoid(za[:, D_CONF:])
    zb = _dot(h, win_ref[:, _C_SB:_C_GA])
    vbuf[8:8 + t, :] = zb[:, D_SHORT:2 * D_SHORT] * zb[:, 2 * D_SHORT:]

    off_b = 8 - (SHORT_KERNEL - 1)
    for c in range(t // _MIX_R):
        base = c * _MIX_R
        blocks = []
        for lb in range(0, D_CONF, 128):
            acc = None
            for j in range(8):
                g = None
                for q in range((CONF_KERNEL - 1 - j) // 8 + 1):
                    k = CONF_KERNEL - 1 - 8 * q - j
                    s = base + halo - 8 - 8 * q
                    term = abuf[s:s + _MIX_R + 8, lb:lb + 128] * ca_ref[k:k + 1, lb:lb + 128]
                    g = term if g is None else g + term
                g = g[8 - j:8 - j + _MIX_R, :]
                acc = g if acc is None else acc + g
            blocks.append(acc)
        acc = jnp.concatenate(blocks, axis=-1) + cab_ref[...]
        mu = jnp.mean(acc, axis=-1, keepdims=True)
        cen = acc - mu
        var = jnp.mean(cen * cen, axis=-1, keepdims=True)
        y = cen * lax.rsqrt(var + LN_EPS) * lng_ref[...] + lnb_ref[...]
        actbuf[base:base + _MIX_R, :] = (y * _sigmoid(y)).astype(_BF)

        u = vbuf[base + off_b:base + off_b + _MIX_R, :] * cb_ref[0:1, :]
        for k in range(1, SHORT_KERNEL):
            s = base + off_b + k
            u = u + vbuf[s:s + _MIX_R, :] * cb_ref[k:k + 1, :]
        ubuf[base:base + _MIX_R, :] = (zb[base:base + _MIX_R, :D_SHORT] * u).astype(_BF)

    abuf[0:halo, :] = abuf[t:t + halo, :]
    vbuf[0:8, :] = vbuf[t:t + 8, :]

    y_a = _dot(actbuf[...], wa_ref[...])
    y_b = _dot(ubuf[...], wb_ref[...])
    zg = _dot(h, win_ref[:, _C_GA:_C_END])
    merged = _sigmoid(zg[:, :D_MODEL]) * y_a + _sigmoid(zg[:, D_MODEL:]) * y_b
    o_ref[...] = x + _dot(merged.astype(_BF), wo_ref[...])


def _mixer(x, norm1, w_in, conv_a, conv_a_bias, ln_g, ln_b, w_a, conv_b, w_b, w_o):
    s = x.shape[0]
    t = _MIX_T
    row = lambda v: v.reshape(1, -1).astype(_F32)
    return pl.pallas_call(
        _mixer_kernel,
        grid=(s // t,),
        in_specs=[
            pl.BlockSpec((t, D_MODEL), lambda i: (i, 0)),
            _const_spec((1, D_MODEL)),
            _const_spec((D_MODEL, _C_END)),
            _const_spec((CONF_KERNEL, D_CONF)),
            _const_spec((1, D_CONF)),
            _const_spec((1, D_CONF)),
            _const_spec((1, D_CONF)),
            _const_spec((D_CONF, D_MODEL)),
            _const_spec((SHORT_KERNEL, D_SHORT)),
            _const_spec((D_SHORT, D_MODEL)),
            _const_spec((D_MODEL, D_MODEL)),
        ],
        out_specs=pl.BlockSpec((t, D_MODEL), lambda i: (i, 0)),
        out_shape=jax.ShapeDtypeStruct((s, D_MODEL), _F32),
        scratch_shapes=[
            pltpu.VMEM((_MIX_HALO + t, D_CONF), _F32),
            pltpu.VMEM((8 + t, D_SHORT), _F32),
            pltpu.VMEM((t, D_CONF), _BF),
            pltpu.VMEM((t, D_SHORT), _BF),
        ],
        compiler_params=pltpu.CompilerParams(
            dimension_semantics=("arbitrary",), vmem_limit_bytes=_VMEM_LIMIT),
        name="mixer",
    )(x, row(norm1), w_in.astype(_BF), conv_a, row(conv_a_bias), row(ln_g), row(ln_b),
      w_a.astype(_BF), conv_b, w_b.astype(_BF), w_o.astype(_BF))


def _ffn_kernel(x_ref, n2_ref, w1_ref, w3_ref, w2_ref, o_ref):
    x = x_ref[...]
    h = _rms_scale(x, n2_ref[...]).astype(_BF)
    g = _dot(h, w1_ref[...])
    u = _dot(h, w3_ref[...])
    act = (g * _sigmoid(g) * u).astype(_BF)
    o_ref[...] = x + _dot(act, w2_ref[...])


def _ffn_dense(x, norm2, w1, w3, w2):
    s = x.shape[0]
    t = _FFN_T
    f = w1.shape[1]
    return pl.pallas_call(
        _ffn_kernel,
        grid=(s // t,),
        in_specs=[
            pl.BlockSpec((t, D_MODEL), lambda i: (i, 0)),
            _const_spec((1, D_MODEL)),
            _const_spec((D_MODEL, f)),
            _const_spec((D_MODEL, f)),
            _const_spec((f, D_MODEL)),
        ],
        out_specs=pl.BlockSpec((t, D_MODEL), lambda i: (i, 0)),
        out_shape=jax.ShapeDtypeStruct((s, D_MODEL), _F32),
        compiler_params=pltpu.CompilerParams(
            dimension_semantics=("arbitrary",), vmem_limit_bytes=_VMEM_LIMIT),
        name="ffn_dense",
    )(x, norm2.reshape(1, -1), w1.astype(_BF), w3.astype(_BF), w2.astype(_BF))


def _route_kernel(x_ref, n2_ref, r_ref, h_ref, meta_ref, cnt_ref, lmat, carry):
    t = _RT_T

    @pl.when(pl.program_id(0) == 0)
    def _():
        row = lax.broadcasted_iota(jnp.int32, (t, t), 0)
        col = lax.broadcasted_iota(jnp.int32, (t, t), 1)
        lmat[...] = jnp.where(col < row, 1.0, 0.0).astype(_BF)
        carry[...] = jnp.zeros(carry.shape, _F32)

    h = _rms_scale(x_ref[...], n2_ref[...])
    h_ref[...] = h
    r = r_ref[...]
    h_hi = h.astype(_BF)
    h_lo = (h - h_hi.astype(_F32)).astype(_BF)
    r_hi = r.astype(_BF)
    r_lo = (r - r_hi.astype(_F32)).astype(_BF)
    logits = _dot(h_hi, r_hi) + (_dot(h_hi, r_lo) + _dot(h_lo, r_hi))
    lane = lax.broadcasted_iota(jnp.int32, logits.shape, 1)
    m1 = jnp.max(logits, axis=-1, keepdims=True)
    i1 = jnp.min(jnp.where(logits == m1, lane, N_EXPERTS), axis=-1, keepdims=True)
    rest = jnp.where(lane == i1, -jnp.inf, logits)
    m2 = jnp.max(rest, axis=-1, keepdims=True)
    i2 = jnp.min(jnp.where(rest == m2, lane, N_EXPERTS), axis=-1, keepdims=True)
    tt = jnp.exp(m2 - m1)
    p1 = 1.0 / (1.0 + tt)
    p2 = tt / (1.0 + tt)

    sel1 = lane == i1
    sel2 = lane == i2
    onehot = jnp.where(sel1, 1.0, 0.0) + jnp.where(sel2, 1.0, 0.0)
    before = _dot(lmat[...], onehot.astype(_BF)) + carry[...]
    rank1 = jnp.sum(jnp.where(sel1, before, 0.0), axis=-1, keepdims=True)
    rank2 = jnp.sum(jnp.where(sel2, before, 0.0), axis=-1, keepdims=True)
    cols = (rank1, rank2, p1, p2, i1.astype(_F32), i2.astype(_F32))
    meta = jnp.zeros(logits.shape, _F32)
    for c, v in enumerate(cols):
        meta = jnp.where(lane == c, v, meta)
    meta_ref[...] = meta
    carry[...] += jnp.sum(onehot, axis=0, keepdims=True)
    cnt_ref[...] = carry[...]


def _route(x, norm2, router):
    s = x.shape[0]
    t = _RT_T
    return pl.pallas_call(
        _route_kernel,
        grid=(s // t,),
        in_specs=[
            pl.BlockSpec((t, D_MODEL), lambda i: (i, 0)),
            _const_spec((1, D_MODEL)),
            _const_spec((D_MODEL, N_EXPERTS)),
        ],
        out_specs=[
            pl.BlockSpec((t, D_MODEL), lambda i: (i, 0)),
            pl.BlockSpec((t, N_EXPERTS), lambda i: (i, 0)),
            pl.BlockSpec((1, N_EXPERTS), lambda i: (0, 0)),
        ],
        out_shape=[
            jax.ShapeDtypeStruct((s, D_MODEL), _F32),
            jax.ShapeDtypeStruct((s, N_EXPERTS), _F32),
            jax.ShapeDtypeStruct((1, N_EXPERTS), _F32),
        ],
        scratch_shapes=[pltpu.VMEM((t, t), _BF), pltpu.VMEM((1, N_EXPERTS), _F32)],
        compiler_params=pltpu.CompilerParams(
            dimension_semantics=("arbitrary",), vmem_limit_bytes=_VMEM_LIMIT),
        name="route",
    )(x, norm2.reshape(1, -1), router)


def _dispatch_kernel(plan_ref, slot_hbm, h_ref, xs_hbm, slot_smem, zbuf, sem_idx, sem_row, sem_z,
                     *, max_tiles):
    t = _DSP_T
    tm = _EXP_TM
    i = pl.program_id(0)
    idx_copy = pltpu.make_async_copy(
        slot_hbm.at[pl.ds(i * TOP_K * t, TOP_K * t)], slot_smem, sem_idx)
    idx_copy.start()

    @pl.when(i == 0)
    def _():
        zbuf[...] = jnp.zeros(zbuf.shape, _F32)

        def zero_row(r, carry):
            pltpu.make_async_copy(zbuf.at[pl.ds(0, 1), :], xs_hbm.at[pl.ds(r, 1), :], sem_z).start()
            return carry

        def wait_row(r, carry):
            pltpu.make_async_copy(zbuf.at[pl.ds(0, 1), :], xs_hbm.at[pl.ds(0, 1), :], sem_z).wait()
            return carry

        def zero_tile(j, carry):
            pltpu.make_async_copy(
                zbuf, xs_hbm.at[pl.ds(pl.multiple_of(j * tm, tm), tm), :], sem_z).start()
            return carry

        def wait_tile(j, carry):
            pltpu.make_async_copy(zbuf, xs_hbm.at[pl.ds(0, tm), :], sem_z).wait()
            return carry

        for e in range(N_EXPERTS):
            lo = plan_ref[N_EXPERTS + e] + plan_ref[e]
            hi = plan_ref[N_EXPERTS + e] + ((plan_ref[e] + tm - 1) // tm) * tm
            lax.fori_loop(lo, hi, zero_row, 0)
            lax.fori_loop(lo, hi, wait_row, 0)
        n_tiles = plan_ref[2 * N_EXPERTS]
        lax.fori_loop(n_tiles, max_tiles, zero_tile, 0)
        lax.fori_loop(n_tiles, max_tiles, wait_tile, 0)

    idx_copy.wait()

    def issue(r, carry):
        for k in range(TOP_K):
            pltpu.make_async_copy(
                h_ref.at[pl.ds(r, 1), :], xs_hbm.at[pl.ds(slot_smem[TOP_K * r + k], 1), :],
                sem_row).start(priority=k)
        return carry

    lax.fori_loop(0, t, issue, 0, unroll=8)
    for k in range(TOP_K):
        pltpu.make_async_copy(h_ref, xs_hbm.at[pl.ds(0, t), :], sem_row).wait()


def _dispatch(h, slots, plan, max_tiles):
    s = h.shape[0]
    t = _DSP_T
    grid_spec = pltpu.PrefetchScalarGridSpec(
        num_scalar_prefetch=1,
        grid=(s // t,),
        in_specs=[
            pl.BlockSpec(memory_space=pl.ANY),
            pl.BlockSpec((t, D_MODEL), lambda i, plan: (i, 0)),
        ],
        out_specs=pl.BlockSpec(memory_space=pl.ANY),
        scratch_shapes=[
            pltpu.SMEM((TOP_K * t,), jnp.int32),
            pltpu.VMEM((_EXP_TM, D_MODEL), _F32),
            pltpu.SemaphoreType.DMA,
            pltpu.SemaphoreType.DMA,
            pltpu.SemaphoreType.DMA,
        ],
    )
    return pl.pallas_call(
        functools.partial(_dispatch_kernel, max_tiles=max_tiles),
        grid_spec=grid_spec,
        out_shape=jax.ShapeDtypeStruct((max_tiles * _EXP_TM, D_MODEL), _F32),
        compiler_params=pltpu.CompilerParams(
            dimension_semantics=("arbitrary",), vmem_limit_bytes=_VMEM_LIMIT),
        name="dispatch",
    )(plan, slots, h)


def _experts_kernel(exp_ref, nt_ref, xs_ref, w1_ref, w3_ref, w2_ref, ys_ref, hbuf, acc):
    i = pl.program_id(0)
    f = pl.program_id(1)
    nf = pl.num_programs(1)

    @pl.when(i < nt_ref[0])
    def _():
        @pl.when(f == 0)
        def _():
            hbuf[...] = xs_ref[...].astype(_BF)

        h = hbuf[...]
        g = _dot(h, w1_ref[...])
        u = _dot(h, w3_ref[...])
        part = _dot((g * _sigmoid(g) * u).astype(_BF), w2_ref[...])

        @pl.when(f == 0)
        def _():
            acc[...] = part

        @pl.when(f > 0)
        def _():
            acc[...] += part

        @pl.when(f == nf - 1)
        def _():
            ys_ref[...] = acc[...]

    @pl.when((i >= nt_ref[0]) & (f == 0))
    def _():
        ys_ref[...] = jnp.zeros(ys_ref.shape, _F32)


def _experts(xs, w1, w3, w2, tile_exp, n_tiles, max_tiles):
    tm = _EXP_TM
    fc = _EXP_F
    ne, _, ff = w1.shape
    nf = ff // fc

    def xs_map(i, f, exp, nt):
        return (jnp.minimum(i, nt[0] - 1), 0)

    def w13_map(i, f, exp, nt):
        return (exp[i], 0, jnp.where(i < nt[0], f, nf - 1))

    def w2_map(i, f, exp, nt):
        return (exp[i], jnp.where(i < nt[0], f, nf - 1), 0)

    grid_spec = pltpu.PrefetchScalarGridSpec(
        num_scalar_prefetch=2,
        grid=(max_tiles, nf),
        in_specs=[
            pl.BlockSpec((tm, D_MODEL), xs_map),
            pl.BlockSpec((None, D_MODEL, fc), w13_map),
            pl.BlockSpec((None, D_MODEL, fc), w13_map),
            pl.BlockSpec((None, fc, D_MODEL), w2_map),
        ],
        out_specs=pl.BlockSpec((tm, D_MODEL), lambda i, f, exp, nt: (i, 0)),
        scratch_shapes=[pltpu.VMEM((tm, D_MODEL), _BF), pltpu.VMEM((tm, D_MODEL), _F32)],
    )
    return pl.pallas_call(
        _experts_kernel,
        grid_spec=grid_spec,
        out_shape=jax.ShapeDtypeStruct(xs.shape, _F32),
        compiler_params=pltpu.CompilerParams(
            dimension_semantics=("arbitrary", "arbitrary"), vmem_limit_bytes=_VMEM_LIMIT),
        name="experts",
    )(tile_exp, n_tiles, xs, w1.astype(_BF), w3.astype(_BF), w2.astype(_BF))


def _combine_kernel(slot_hbm, x_ref, meta_ref, fn_ref, ys_hbm, o_ref, slot_smem, gbuf, sem_idx,
                    sem_row):
    t = _CMB_T
    i = pl.program_id(0)
    idx_copy = pltpu.make_async_copy(
        slot_hbm.at[pl.ds(i * TOP_K * t, TOP_K * t)], slot_smem, sem_idx)
    idx_copy.start()
    idx_copy.wait()

    def issue(r, carry):
        for k in range(TOP_K):
            pltpu.make_async_copy(
                ys_hbm.at[pl.ds(slot_smem[TOP_K * r + k], 1), :], gbuf.at[k, pl.ds(r, 1), :],
                sem_row).start(priority=k)
        return carry

    lax.fori_loop(0, t, issue, 0, unroll=8)
    for k in range(TOP_K):
        pltpu.make_async_copy(ys_hbm.at[pl.ds(0, t), :], gbuf.at[k], sem_row).wait()

    meta = meta_ref[...]
    y = x_ref[...] + meta[:, 2:3] * gbuf[0] + meta[:, 3:4] * gbuf[1]
    o_ref[...] = _rms_scale(y, fn_ref[...])


def _combine(x, meta, slots, ys, final_norm):
    s = x.shape[0]
    t = _CMB_T
    return pl.pallas_call(
        _combine_kernel,
        grid=(s // t,),
        in_specs=[
            pl.BlockSpec(memory_space=pl.ANY),
            pl.BlockSpec((t, D_MODEL), lambda i: (i, 0)),
            pl.BlockSpec((t, N_EXPERTS), lambda i: (i, 0)),
            _const_spec((1, D_MODEL)),
            pl.BlockSpec(memory_space=pl.ANY),
        ],
        out_specs=pl.BlockSpec((t, D_MODEL), lambda i: (i, 0)),
        out_shape=jax.ShapeDtypeStruct((s, D_MODEL), _F32),
        scratch_shapes=[
            pltpu.SMEM((TOP_K * t,), jnp.int32),
            pltpu.VMEM((TOP_K, t, D_MODEL), _F32),
            pltpu.SemaphoreType.DMA,
            pltpu.SemaphoreType.DMA,
        ],
        compiler_params=pltpu.CompilerParams(
            dimension_semantics=("arbitrary",), vmem_limit_bytes=_VMEM_LIMIT),
        name="combine",
    )(slots, x, meta, final_norm.reshape(1, -1), ys)


def _moe_routed(x, norm2, router, w1, w3, w2, final_norm):
    s = x.shape[0]
    tm = _EXP_TM
    max_tiles = (TOP_K * s) // tm + N_EXPERTS

    h, meta, counts = _route(x, norm2, router)

    counts = counts.reshape(N_EXPERTS).astype(jnp.int32)
    tiles_per_expert = (counts + tm - 1) // tm
    tile_end = jnp.cumsum(tiles_per_expert)
    row_start = (tile_end - tiles_per_expert) * tm
    n_tiles = tile_end[-1:]
    tile_id = jnp.minimum(jnp.arange(max_tiles, dtype=jnp.int32), n_tiles[0] - 1)
    tile_exp = jnp.sum(tile_id[:, None] >= tile_end[None, :], axis=1).astype(jnp.int32)
    slots = meta[:, :TOP_K].astype(jnp.int32) + row_start[meta[:, 4:4 + TOP_K].astype(jnp.int32)]
    slots = slots.reshape(TOP_K * s)
    plan = jnp.concatenate([counts, row_start, n_tiles]).astype(jnp.int32)

    xs = _dispatch(h, slots, plan, max_tiles)
    ys = _experts(xs, w1, w3, w2, tile_exp, n_tiles.astype(jnp.int32), max_tiles)
    return _combine(x, meta, slots, ys, final_norm)


def kernel(x, l0_norm1, l0_w_in, l0_conv_a, l0_conv_a_bias, l0_ln_a_g, l0_ln_a_b, l0_w_a, l0_conv_b, l0_w_b, l0_w_o, l0_norm2, l0_ffn_w1, l0_ffn_w3, l0_ffn_w2, l1_norm1, l1_w_in, l1_conv_a, l1_conv_a_bias, l1_ln_a_g, l1_ln_a_b, l1_w_a, l1_conv_b, l1_w_b, l1_w_o, l1_norm2, l1_router, l1_moe_w1, l1_moe_w3, l1_moe_w2, final_norm):
    b, s, d = x.shape
    assert b == 1 and d == D_MODEL, "the causal conv history is carried across token tiles of one sequence"
    xs = x.reshape(s, d)
    xs = _mixer(xs, l0_norm1, l0_w_in, l0_conv_a, l0_conv_a_bias, l0_ln_a_g, l0_ln_a_b, l0_w_a,
                l0_conv_b, l0_w_b, l0_w_o)
    xs = _ffn_dense(xs, l0_norm2, l0_ffn_w1, l0_ffn_w3, l0_ffn_w2)
    xs = _mixer(xs, l1_norm1, l1_w_in, l1_conv_a, l1_conv_a_bias, l1_ln_a_g, l1_ln_a_b, l1_w_a,
                l1_conv_b, l1_w_b, l1_w_o)
    out = _moe_routed(xs, l1_norm2, l1_router, l1_moe_w1, l1_moe_w3, l1_moe_w2, final_norm)
    return out.reshape(b, s, d)
```
